```python
import jax, jax.numpy as jnp
from jax import lax
import numpy as np

D_MODEL = 1024
BATCH = 4
SEQ = 8192
DEPTH = 4

N_MIXERS = 2
EPS = 1e-6

CHUNK = 128
A_WIDTH = 2 * D_MODEL
A_GROUPS = 8
A_GROUP_DIM = A_WIDTH // A_GROUPS

B_WINDOWS = (2, 4, 8, 16)
B_GROUPS = len(B_WINDOWS)
B_WIDTH = D_MODEL
B_GROUP_DIM = B_WIDTH // B_GROUPS

D_FF = ((8 * D_MODEL + 3 * 256 - 1) // (3 * 256)) * 256

N_A_LAYERS = (DEPTH + 1) // 2
N_B_LAYERS = DEPTH // 2

kernel_name = 'hybrid_gmlp_pool_swiglu_trunk'


def rmsnorm(x, g):
    xf = x.astype(jnp.float32)
    y = xf * lax.rsqrt(jnp.mean(xf * xf, axis=-1, keepdims=True) + EPS)
    return (y * g.astype(jnp.float32)).astype(x.dtype)


def layernorm(x, g, b):
    xf = x.astype(jnp.float32)
    mu = jnp.mean(xf, axis=-1, keepdims=True)
    xc = xf - mu
    var = jnp.mean(xc * xc, axis=-1, keepdims=True)
    y = xc * lax.rsqrt(var + EPS) * g.astype(jnp.float32) + b.astype(jnp.float32)
    return y.astype(x.dtype)


def mixer_a(h, w_in, ln_g, ln_b, w_s, b_s, w_out):
    bsz, s, _ = h.shape
    z = jax.nn.gelu(h @ w_in, approximate=False)
    u, v = jnp.split(z, 2, axis=-1)
    v = layernorm(v, ln_g, ln_b)
    n_chunks = s // CHUNK
    v = v.reshape(bsz, n_chunks, CHUNK, A_GROUPS, A_GROUP_DIM)
    u = u.reshape(bsz, n_chunks, CHUNK, A_GROUPS, A_GROUP_DIM)
    causal = jnp.tril(jnp.ones((CHUNK, CHUNK), dtype=bool))
    w = jnp.where(causal[None], w_s, jnp.zeros_like(w_s))
    sv = jnp.einsum('gts,bnsgd->bntgd', w, v)
    sv = sv + jnp.transpose(b_s)[None, None, :, :, None]
    gated = (u * sv).reshape(bsz, s, A_WIDTH)
    return gated @ w_out


def mixer_b(h, w_in, w_grp, scale, w_out):
    bsz, s, _ = h.shape
    p = h @ w_in
    pf = p.astype(jnp.float32)
    cs = jnp.cumsum(pf, axis=1)
    cs0 = jnp.concatenate([jnp.zeros((bsz, 1, B_WIDTH), jnp.float32), cs], axis=1)
    t = jnp.arange(s)
    pooled = []
    for g, win in enumerate(B_WINDOWS):
        lo, hi = g * B_GROUP_DIM, (g + 1) * B_GROUP_DIM
        c = cs0[..., lo:hi]
        c_pad = jnp.concatenate([jnp.zeros((bsz, win - 1, B_GROUP_DIM), jnp.float32), c], axis=1)
        total = c[:, 1:] - c_pad[:, :s]
        count = jnp.minimum(t + 1, win).astype(jnp.float32)
        pooled.append(total / count[None, :, None] - pf[..., lo:hi])
    pooled = jnp.stack(pooled, axis=2)
    mixed = jnp.einsum('bsgd,gde->bsge', pooled, w_grp.astype(jnp.float32))
    mixed = mixed.reshape(bsz, s, B_WIDTH) * scale.astype(jnp.float32)
    return mixed.astype(h.dtype) @ w_out


def swiglu(h, w_gate, w_up, w_down):
    return (jax.nn.silu(h @ w_gate) * (h @ w_up)) @ w_down


def setup_inputs(seed: int = 0) -> dict:
    key = jax.random.key(seed)
    ks = jax.random.split(key, 20)
    f32 = jnp.float32
    d = D_MODEL
    x = jax.random.normal(ks[0], (BATCH, SEQ, d), f32)
    a_w_in = jax.random.normal(ks[1], (N_A_LAYERS, d, 2 * A_WIDTH), f32) * d ** -0.5
    a_ln_g = 1.0 + 0.02 * jax.random.normal(ks[2], (N_A_LAYERS, A_WIDTH), f32)
    a_ln_b = 0.02 * jax.random.normal(ks[3], (N_A_LAYERS, A_WIDTH), f32)
    a_w_s = jnp.tril(jax.random.normal(ks[4], (N_A_LAYERS, A_GROUPS, CHUNK, CHUNK), f32) * CHUNK ** -0.5)
    a_b_s = 1.0 + 0.1 * jax.random.normal(ks[5], (N_A_LAYERS, A_GROUPS, CHUNK), f32)
    a_w_out = jax.random.normal(ks[6], (N_A_LAYERS, A_WIDTH, d), f32) * A_WIDTH ** -0.5
    b_w_in = jax.random.normal(ks[7], (N_B_LAYERS, d, B_WIDTH), f32) * d ** -0.5
    b_w_grp = jax.random.normal(ks[8], (N_B_LAYERS, B_GROUPS, B_GROUP_DIM, B_GROUP_DIM), f32) * B_GROUP_DIM ** -0.5
    b_scale = 1.0 + 0.1 * jax.random.normal(ks[9], (N_B_LAYERS, B_WIDTH), f32)
    b_w_out = jax.random.normal(ks[10], (N_B_LAYERS, B_WIDTH, d), f32) * B_WIDTH ** -0.5
    mix_pre_g = 1.0 + 0.02 * jax.random.normal(ks[11], (DEPTH, d), f32)
    mix_post_g = 1.0 + 0.02 * jax.random.normal(ks[12], (DEPTH, d), f32)
    ffn_pre_g = 1.0 + 0.02 * jax.random.normal(ks[13], (DEPTH, d), f32)
    ffn_post_g = 1.0 + 0.02 * jax.random.normal(ks[14], (DEPTH, d), f32)
    ffn_w_gate = jax.random.normal(ks[15], (DEPTH, d, D_FF), f32) * d ** -0.5
    ffn_w_up = jax.random.normal(ks[16], (DEPTH, d, D_FF), f32) * d ** -0.5
    ffn_w_down = jax.random.normal(ks[17], (DEPTH, D_FF, d), f32) * D_FF ** -0.5
    return {'x': x, 'a_w_in': a_w_in, 'a_ln_g': a_ln_g, 'a_ln_b': a_ln_b,
            'a_w_s': a_w_s, 'a_b_s': a_b_s, 'a_w_out': a_w_out,
            'b_w_in': b_w_in, 'b_w_grp': b_w_grp, 'b_scale': b_scale, 'b_w_out': b_w_out,
            'mix_pre_g': mix_pre_g, 'mix_post_g': mix_post_g,
            'ffn_pre_g': ffn_pre_g, 'ffn_post_g': ffn_post_g,
            'ffn_w_gate': ffn_w_gate, 'ffn_w_up': ffn_w_up, 'ffn_w_down': ffn_w_down}


def reference(x, a_w_in, a_ln_g, a_ln_b, a_w_s, a_b_s, a_w_out,
              b_w_in, b_w_grp, b_scale, b_w_out,
              mix_pre_g, mix_post_g, ffn_pre_g, ffn_post_g,
              ffn_w_gate, ffn_w_up, ffn_w_down):
    for i in range(DEPTH):
        j = i // N_MIXERS
        h = rmsnorm(x, mix_pre_g[i])
        if i % N_MIXERS == 0:
            m = mixer_a(h, a_w_in[j], a_ln_g[j], a_ln_b[j], a_w_s[j], a_b_s[j], a_w_out[j])
        else:
            m = mixer_b(h, b_w_in[j], b_w_grp[j], b_scale[j], b_w_out[j])
        x = x + rmsnorm(m, mix_post_g[i])
        h = rmsnorm(x, ffn_pre_g[i])
        f = swiglu(h, ffn_w_gate[i], ffn_w_up[i], ffn_w_down[i])
        x = x + rmsnorm(f, ffn_post_g[i])
    return x
```

```python
import functools

import jax
import jax.numpy as jnp
from jax import lax
from jax.experimental import pallas as pl
from jax.experimental.pallas import tpu as pltpu

EPS = 1e-6
B_WINDOWS = (2, 4, 8, 16)

V7X_VMEM_BYTES = 64 * 1024 * 1024
V7X_SUBLANES = 8
V7X_MXU_DIM = 256

ROW_TILE = 512
POOL_HALO = 32

_F32 = jnp.float32
_BF16 = jnp.bfloat16


def _rms_scale(x, g):
    ms = jnp.mean(x * x, axis=-1, keepdims=True)
    return x * lax.rsqrt(ms + EPS) * g


def _gelu_exact(x):
    return 0.5 * x * (1.0 + lax.erf(x * (0.5 ** 0.5)))


def _dot(a, b):
    return jnp.dot(a, b, preferred_element_type=_F32)


def _ffn_kernel(x_ref, gpre_ref, gpost_ref, wgu_ref, wd_ref, o_ref, h_ref, a_ref):
    n_chunks, _, two_chunk = wgu_ref.shape
    chunk = two_chunk // 2
    h_ref[...] = _rms_scale(x_ref[...], gpre_ref[...]).astype(_BF16)
    for c in range(n_chunks):
        r = _dot(h_ref[...], wgu_ref[c])
        g = r[:, :chunk]
        u = r[:, chunk:]
        a_ref[:, c * chunk:(c + 1) * chunk] = (g * jax.nn.sigmoid(g) * u).astype(_BF16)
    f = _dot(a_ref[...], wd_ref[...])
    o_ref[...] = x_ref[...] + _rms_scale(f, gpost_ref[...])


def _mixer_a_kernel(x_ref, gpre_ref, gpost_ref, wv_ref, wu_ref, lng_ref, lnb_ref,
                    ws_ref, bs_ref, wout_ref, o_ref,
                    h_ref, v_ref, vn_ref, gated_ref):
    rows = x_ref.shape[0]
    n_vchunks, _, vchunk = wv_ref.shape
    groups, _, gdim = wu_ref.shape
    t_chunk = ws_ref.shape[-1]
    width = n_vchunks * vchunk

    h_ref[...] = _rms_scale(x_ref[...], gpre_ref[...]).astype(_BF16)

    rowsum = jnp.zeros((rows, 1), _F32)
    for c in range(n_vchunks):
        vc = _gelu_exact(_dot(h_ref[...], wv_ref[c]))
        v_ref[:, c * vchunk:(c + 1) * vchunk] = vc
        rowsum = rowsum + jnp.sum(vc, axis=-1, keepdims=True)
    mu = rowsum * (1.0 / width)
    sq = jnp.zeros((rows, 1), _F32)
    for c in range(n_vchunks):
        xc = v_ref[:, c * vchunk:(c + 1) * vchunk] - mu
        sq = sq + jnp.sum(xc * xc, axis=-1, keepdims=True)
    rstd = lax.rsqrt(sq * (1.0 / width) + EPS)
    for c in range(n_vchunks):
        sl = slice(c * vchunk, (c + 1) * vchunk)
        vn = (v_ref[:, sl] - mu) * rstd * lng_ref[:, sl] + lnb_ref[:, sl]
        vn_ref[:, sl] = vn.astype(_BF16)

    r_id = lax.broadcasted_iota(jnp.int32, (t_chunk, t_chunk), 0)
    c_id = lax.broadcasted_iota(jnp.int32, (t_chunk, t_chunk), 1)
    causal = r_id >= c_id
    for g in range(groups):
        gsl = slice(g * gdim, (g + 1) * gdim)
        w_tril = jnp.where(causal, ws_ref[g], 0.0).astype(_BF16)
        bias = bs_ref[:, g:g + 1]
        ug = _gelu_exact(_dot(h_ref[...], wu_ref[g]))
        for n in range(rows // t_chunk):
            rsl = slice(n * t_chunk, (n + 1) * t_chunk)
            sv = _dot(w_tril, vn_ref[rsl, gsl]) + bias
            gated_ref[rsl, gsl] = (ug[rsl, :] * sv).astype(_BF16)

    m = _dot(gated_ref[...], wout_ref[...])
    o_ref[...] = x_ref[...] + _rms_scale(m, gpost_ref[...])


def _mixer_b_kernel(x_ref, gpre_ref, gpost_ref, win_ref, wgrp_ref, scale_ref, wout_ref,
                    o_ref, p_ref, s0_ref, s1_ref, mixed_ref):
    rows = x_ref.shape[0]
    groups, gdim, _ = wgrp_ref.shape
    halo = p_ref.shape[0] - rows
    t_idx = pl.program_id(1)

    @pl.when(t_idx == 0)
    def _():
        p_ref[0:halo, :] = jnp.zeros((halo, p_ref.shape[1]), _F32)

    h = _rms_scale(x_ref[...], gpre_ref[...]).astype(_BF16)
    p_ref[halo:, :] = _dot(h, win_ref[...])

    pos = t_idx * rows + lax.broadcasted_iota(jnp.int32, (rows, 1), 0)
    for g in range(groups):
        win = B_WINDOWS[g]
        gsl = slice(g * gdim, (g + 1) * gdim)
        src, dst = None, s0_ref
        shift, lo = 1, V7X_SUBLANES
        while shift < win:
            n = rows + halo - lo
            if src is None:
                cur = p_ref[pl.ds(lo, n), gsl]
                prev = p_ref[pl.ds(lo - shift, n), gsl]
            else:
                cur = src[pl.ds(lo, n), :]
                prev = src[pl.ds(lo - shift, n), :]
            dst[pl.ds(lo, n), :] = cur + prev
            src, dst = dst, (s1_ref if dst is s0_ref else s0_ref)
            shift *= 2
            lo += V7X_SUBLANES
        total = src[pl.ds(halo, rows), :]
        count = jnp.minimum(pos + 1, win).astype(_F32)
        pooled = total / count - p_ref[pl.ds(halo, rows), gsl]
        mixed = _dot(pooled.astype(_BF16), wgrp_ref[g]) * scale_ref[:, gsl]
        mixed_ref[:, gsl] = mixed.astype(_BF16)

    p_ref[0:halo, :] = p_ref[rows:rows + halo, :]
    m = _dot(mixed_ref[...], wout_ref[...])
    o_ref[...] = x_ref[...] + _rms_scale(m, gpost_ref[...])


def _resident(shape):
    zeros = (0,) * len(shape)
    return pl.BlockSpec(shape, lambda *_: zeros, pipeline_mode=pl.Buffered(1))


def _nbytes(shape, dtype):
    n = 1
    for s in shape:
        n *= s
    return n * jnp.dtype(dtype).itemsize


def _vmem_limit(resident_bytes, rows, d, scratch_bytes, temp_bytes):
    io = 2 * 2 * _nbytes((rows, d), _F32)
    est = resident_bytes + io + scratch_bytes + temp_bytes
    assert est <= V7X_VMEM_BYTES, est
    return est


def _ffn_call(x2d, gpre, gpost, wgu, wd, rows):
    n, d = x2d.shape
    d_ff = wd.shape[0]
    row_spec = pl.BlockSpec((rows, d), lambda i: (i, 0))
    scratch = [pltpu.VMEM((rows, d), _BF16), pltpu.VMEM((rows, d_ff), _BF16)]
    resident = _nbytes(wgu.shape, _BF16) + _nbytes(wd.shape, _BF16) + 2 * _nbytes((8, d), _F32)
    scratch_bytes = _nbytes((rows, d), _BF16) + _nbytes((rows, d_ff), _BF16)
    temps = 4 * _nbytes((rows, d), _F32)
    return pl.pallas_call(
        _ffn_kernel,
        grid=(n // rows,),
        in_specs=[row_spec, _resident(gpre.shape), _resident(gpost.shape),
                  _resident(wgu.shape), _resident(wd.shape)],
        out_specs=row_spec,
        out_shape=jax.ShapeDtypeStruct((n, d), x2d.dtype),
        scratch_shapes=scratch,
        compiler_params=pltpu.CompilerParams(
            dimension_semantics=("parallel",),
            vmem_limit_bytes=_vmem_limit(resident, rows, d, scratch_bytes, temps)),
        name="ffn_sublayer",
    )(x2d, gpre, gpost, wgu, wd)


def _mixer_a_call(x2d, gpre, gpost, wv, wu, lng, lnb, ws, bs_t, wout, rows):
    n, d = x2d.shape
    width = wout.shape[0]
    row_spec = pl.BlockSpec((rows, d), lambda i: (i, 0))
    scratch = [pltpu.VMEM((rows, d), _BF16), pltpu.VMEM((rows, width), _F32),
               pltpu.VMEM((rows, width), _BF16), pltpu.VMEM((rows, width), _BF16)]
    resident = (_nbytes(wv.shape, _BF16) + _nbytes(wu.shape, _BF16) + _nbytes(wout.shape, _BF16)
                + _nbytes(ws.shape, _F32) + 5 * _nbytes((8, width), _F32))
    scratch_bytes = (_nbytes((rows, d), _BF16) + _nbytes((rows, width), _F32)
                     + 2 * _nbytes((rows, width), _BF16))
    temps = 4 * _nbytes((rows, d), _F32)
    return pl.pallas_call(
        _mixer_a_kernel,
        grid=(n // rows,),
        in_specs=[row_spec, _resident(gpre.shape), _resident(gpost.shape),
                  _resident(wv.shape), _resident(wu.shape), _resident(lng.shape),
                  _resident(lnb.shape), _resident(ws.shape), _resident(bs_t.shape),
                  _resident(wout.shape)],
        out_specs=row_spec,
        out_shape=jax.ShapeDtypeStruct((n, d), x2d.dtype),
        scratch_shapes=scratch,
        compiler_params=pltpu.CompilerParams(
            dimension_semantics=("parallel",),
            vmem_limit_bytes=_vmem_limit(resident, rows, d, scratch_bytes, temps)),
        name="mixer_a_sublayer",
    )(x2d, gpre, gpost, wv, wu, lng, lnb, ws, bs_t, wout)


def _mixer_b_call(x2d, gpre, gpost, win, wgrp, scale, wout, rows, seq):
    n, d = x2d.shape
    width = wout.shape[0]
    gdim = wgrp.shape[1]
    tiles = seq // rows
    row_spec = pl.BlockSpec((rows, d), lambda b, t: (b * tiles + t, 0))
    scratch = [pltpu.VMEM((rows + POOL_HALO, width), _F32),
               pltpu.VMEM((rows + POOL_HALO, gdim), _F32),
               pltpu.VMEM((rows + POOL_HALO, gdim), _F32),
               pltpu.VMEM((rows, width), _BF16)]
    resident = (_nbytes(win.shape, _BF16) + _nbytes(wgrp.shape, _BF16)
                + _nbytes(wout.shape, _BF16) + 3 * _nbytes((8, d), _F32))
    scratch_bytes = (_nbytes((rows + POOL_HALO, width), _F32)
                     + 2 * _nbytes((rows + POOL_HALO, gdim), _F32)
                     + _nbytes((rows, width), _BF16))
    temps = 4 * _nbytes((rows, d), _F32)
    return pl.pallas_call(
        _mixer_b_kernel,
        grid=(n // seq, tiles),
        in_specs=[row_spec, _resident(gpre.shape), _resident(gpost.shape),
                  _resident(win.shape), _resident(wgrp.shape), _resident(scale.shape),
                  _resident(wout.shape)],
        out_specs=row_spec,
        out_shape=jax.ShapeDtypeStruct((n, d), x2d.dtype),
        scratch_shapes=scratch,
        compiler_params=pltpu.CompilerParams(
            dimension_semantics=("parallel", "arbitrary"),
            vmem_limit_bytes=_vmem_limit(resident, rows, d, scratch_bytes, temps)),
        name="mixer_b_sublayer",
    )(x2d, gpre, gpost, win, wgrp, scale, wout)


def _split_cols(w, chunk):
    k, n = w.shape
    return jnp.transpose(w.reshape(k, n // chunk, chunk), (1, 0, 2))


def kernel(x, a_w_in, a_ln_g, a_ln_b, a_w_s, a_b_s, a_w_out, b_w_in, b_w_grp, b_scale, b_w_out, mix_pre_g, mix_post_g, ffn_pre_g, ffn_post_g, ffn_w_gate, ffn_w_up, ffn_w_down):
    bsz, seq, d = x.shape
    depth = mix_pre_g.shape[0]
    a_width = a_w_out.shape[1]
    a_groups = a_w_s.shape[1]
    a_gdim = a_width // a_groups
    rows = ROW_TILE
    assert seq % rows == 0 and rows % a_w_s.shape[-1] == 0 and rows % V7X_SUBLANES == 0
    assert len(B_WINDOWS) == b_w_grp.shape[1] and max(B_WINDOWS) <= POOL_HALO - V7X_SUBLANES
    assert ffn_w_gate.shape[-1] % V7X_MXU_DIM == 0

    x2d = x.reshape(bsz * seq, d)
    for i in range(depth):
        j = i // 2
        gpre = mix_pre_g[i][None, :]
        gpost = mix_post_g[i][None, :]
        if i % 2 == 0:
            w_in = a_w_in[j].astype(_BF16)
            wu = _split_cols(w_in[:, :a_width], a_gdim)
            wv = _split_cols(w_in[:, a_width:], 2 * V7X_MXU_DIM)
            x2d = _mixer_a_call(
                x2d, gpre, gpost, wv, wu, a_ln_g[j][None, :], a_ln_b[j][None, :],
                a_w_s[j], jnp.transpose(a_b_s[j]), a_w_out[j].astype(_BF16), rows)
        else:
            x2d = _mixer_b_call(
                x2d, gpre, gpost, b_w_in[j].astype(_BF16), b_w_grp[j].astype(_BF16),
                b_scale[j][None, :], b_w_out[j].astype(_BF16), rows, seq)
        wg = _split_cols(ffn_w_gate[i].astype(_BF16), V7X_MXU_DIM)
        wu_f = _split_cols(ffn_w_up[i].astype(_BF16), V7X_MXU_DIM)
        wgu = jnp.concatenate([wg, wu_f], axis=-1)
        x2d = _ffn_call(x2d, ffn_pre_g[i][None, :], ffn_post_g[i][None, :], wgu,
                        ffn_w_down[i].astype(_BF16), rows)
    return x2d.reshape(bsz, seq, d)
```

```python
import jax
import jax.numpy as jnp
from jax import lax
from jax.experimental import pallas as pl
from jax.experimental.pallas import tpu as pltpu

EPS = 1e-6
B_WINDOWS = (2, 4, 8, 16)

V7X_VMEM_BYTES = 64 * 1024 * 1024
V7X_SUBLANES = 8
V7X_MXU_DIM = 256

BLOCK_ROWS = 1024
SUB_ROWS = 512
COL_CHUNK = 2 * V7X_MXU_DIM
POOL_HALO = 32

_F32 = jnp.float32
_BF16 = jnp.bfloat16


def _rms_scale(x, g):
    ms = jnp.mean(x * x, axis=-1, keepdims=True)
    return x * lax.rsqrt(ms + EPS) * g


def _gelu_exact(x):
    return 0.5 * x * (1.0 + lax.erf(x * (0.5 ** 0.5)))


def _dot(a, b):
    return jnp.dot(a, b, preferred_element_type=_F32)


def _col_chunks(n, chunk):
    return [slice(lo, min(lo + chunk, n)) for lo in range(0, n, chunk)]


def _sub_tiles(rows):
    return [slice(lo, lo + SUB_ROWS) for lo in range(0, rows, SUB_ROWS)]


def _ffn_kernel(x_ref, gpre_ref, gpost_ref, wg_ref, wu_ref, wd_ref, o_ref, h_ref, a_ref):
    d_ff = wd_ref.shape[0]
    for rsl in _sub_tiles(x_ref.shape[0]):
        h_ref[rsl, :] = _rms_scale(x_ref[rsl, :], gpre_ref[...]).astype(_BF16)
        for csl in _col_chunks(d_ff, COL_CHUNK):
            g = _dot(h_ref[rsl, :], wg_ref[:, csl])
            u = _dot(h_ref[rsl, :], wu_ref[:, csl])
            a_ref[rsl, csl] = (g * jax.nn.sigmoid(g) * u).astype(_BF16)
        f = _dot(a_ref[rsl, :], wd_ref[...])
        o_ref[rsl, :] = x_ref[rsl, :] + _rms_scale(f, gpost_ref[...])


def _mixer_a_kernel(x_ref, gpre_ref, gpost_ref, win_ref, lng_ref, lnb_ref,
                    ws_ref, bs_ref, wout_ref, o_ref, h_ref, v_ref, gated_ref):
    width = wout_ref.shape[0]
    groups, t_chunk, _ = ws_ref.shape
    gdim = width // groups
    v_chunks = _col_chunks(width, COL_CHUNK)

    r_id = lax.broadcasted_iota(jnp.int32, (t_chunk, t_chunk), 0)
    c_id = lax.broadcasted_iota(jnp.int32, (t_chunk, t_chunk), 1)
    causal = r_id >= c_id

    for rsl in _sub_tiles(x_ref.shape[0]):
        h_ref[rsl, :] = _rms_scale(x_ref[rsl, :], gpre_ref[...]).astype(_BF16)

        rowsum = jnp.zeros((SUB_ROWS, 1), _F32)
        for csl in v_chunks:
            wsl = slice(width + csl.start, width + csl.stop)
            vc = _gelu_exact(_dot(h_ref[rsl, :], win_ref[:, wsl]))
            v_ref[rsl, csl] = vc
            rowsum = rowsum + jnp.sum(vc, axis=-1, keepdims=True)
        mu = rowsum * (1.0 / width)
        sq = jnp.zeros((SUB_ROWS, 1), _F32)
        for csl in v_chunks:
            xc = v_ref[rsl, csl] - mu
            sq = sq + jnp.sum(xc * xc, axis=-1, keepdims=True)
        rstd = lax.rsqrt(sq * (1.0 / width) + EPS)

        for g in range(groups):
            gsl = slice(g * gdim, (g + 1) * gdim)
            w_tril = jnp.where(causal, ws_ref[g], 0.0).astype(_BF16)
            bias = bs_ref[:, g:g + 1]
            vn = ((v_ref[rsl, gsl] - mu) * rstd * lng_ref[:, gsl] + lnb_ref[:, gsl]).astype(_BF16)
            ug = _gelu_exact(_dot(h_ref[rsl, :], win_ref[:, gsl]))
            for n in range(SUB_ROWS // t_chunk):
                tsl = slice(n * t_chunk, (n + 1) * t_chunk)
                sv = _dot(w_tril, vn[tsl, :]) + bias
                osl = slice(rsl.start + tsl.start, rsl.start + tsl.stop)
                gated_ref[osl, gsl] = (ug[tsl, :] * sv).astype(_BF16)

        m = _dot(gated_ref[rsl, :], wout_ref[...])
        o_ref[rsl, :] = x_ref[rsl, :] + _rms_scale(m, gpost_ref[...])


def _mixer_b_kernel(x_ref, gpre_ref, gpost_ref, win_ref, wgrp_ref, scale_ref, wout_ref,
                    o_ref, p_ref, s0_ref, s1_ref, mixed_ref):
    rows = x_ref.shape[0]
    groups, gdim, _ = wgrp_ref.shape
    halo = p_ref.shape[0] - rows
    t_idx = pl.program_id(1)

    @pl.when(t_idx == 0)
    def _():
        p_ref[0:halo, :] = jnp.zeros((halo, p_ref.shape[1]), _F32)

    for rsl in _sub_tiles(rows):
        h = _rms_scale(x_ref[rsl, :], gpre_ref[...]).astype(_BF16)
        p_ref[pl.ds(halo + rsl.start, SUB_ROWS), :] = _dot(h, win_ref[...])

    for s_idx, rsl in enumerate(_sub_tiles(rows)):
        base = rsl.start
        pos = t_idx * rows + base + lax.broadcasted_iota(jnp.int32, (SUB_ROWS, 1), 0)
        s_lo = s_idx * (SUB_ROWS + halo)
        for g in range(groups):
            win = B_WINDOWS[g]
            gsl = slice(g * gdim, (g + 1) * gdim)
            src, dst = None, s0_ref
            shift, lo = 1, V7X_SUBLANES
            while shift < win:
                n = SUB_ROWS + halo - lo
                if src is None:
                    cur = p_ref[pl.ds(base + lo, n), gsl]
                    prev = p_ref[pl.ds(base + lo - shift, n), gsl]
                else:
                    cur = src[pl.ds(s_lo + lo, n), :]
                    prev = src[pl.ds(s_lo + lo - shift, n), :]
                dst[pl.ds(s_lo + lo, n), :] = cur + prev
                src, dst = dst, (s1_ref if dst is s0_ref else s0_ref)
                shift *= 2
                lo += V7X_SUBLANES
            total = src[pl.ds(s_lo + halo, SUB_ROWS), :]
            inv_count = 1.0 / jnp.minimum(pos + 1, win).astype(_F32)
            pooled = total * inv_count - p_ref[pl.ds(base + halo, SUB_ROWS), gsl]
            mixed = _dot(pooled.astype(_BF16), wgrp_ref[g]) * scale_ref[:, gsl]
            mixed_ref[rsl, gsl] = mixed.astype(_BF16)
        m = _dot(mixed_ref[rsl, :], wout_ref[...])
        o_ref[rsl, :] = x_ref[rsl, :] + _rms_scale(m, gpost_ref[...])

    p_ref[0:halo, :] = p_ref[rows:rows + halo, :]


def _resident(shape):
    zeros = (0,) * len(shape)
    return pl.BlockSpec(shape, lambda *_: zeros, pipeline_mode=pl.Buffered(1))


def _nbytes(shape, dtype):
    n = 1
    for s in shape:
        n *= s
    return n * jnp.dtype(dtype).itemsize


def _vmem_limit(resident, scratch, rows, d):
    io = 2 * 2 * _nbytes((rows, d), _F32)
    n_sub = rows // SUB_ROWS
    temps = n_sub * (4 * _nbytes((SUB_ROWS, COL_CHUNK), _F32) + 2 * _nbytes((SUB_ROWS, d), _F32))
    total = sum(_nbytes(s, t) for s, t in resident) + sum(_nbytes(s, t) for s, t in scratch) + io + temps
    assert total <= V7X_VMEM_BYTES, total
    return total


def _sublayer_call(body, name, x2d, residents, scratch, grid, row_map, semantics):
    n, d = x2d.shape
    rows = BLOCK_ROWS
    row_spec = pl.BlockSpec((rows, d), row_map)
    limit = _vmem_limit([(r.shape, r.dtype) for r in residents], scratch, rows, d)
    return pl.pallas_call(
        body,
        grid=grid,
        in_specs=[row_spec] + [_resident(r.shape) for r in residents],
        out_specs=row_spec,
        out_shape=jax.ShapeDtypeStruct((n, d), x2d.dtype),
        scratch_shapes=[pltpu.VMEM(s, t) for s, t in scratch],
        compiler_params=pltpu.CompilerParams(
            dimension_semantics=semantics, vmem_limit_bytes=limit),
        name=name,
    )(x2d, *residents)


def _ffn_call(x2d, gpre, gpost, wg, wu, wd):
    n, d = x2d.shape
    scratch = [((BLOCK_ROWS, d), _BF16), ((BLOCK_ROWS, wd.shape[0]), _BF16)]
    return _sublayer_call(_ffn_kernel, "ffn_sublayer", x2d, [gpre, gpost, wg, wu, wd], scratch,
                          (n // BLOCK_ROWS,), lambda i: (i, 0), ("parallel",))


def _mixer_a_call(x2d, gpre, gpost, win, lng, lnb, ws, bs_t, wout):
    n, d = x2d.shape
    width = wout.shape[0]
    scratch = [((BLOCK_ROWS, d), _BF16), ((BLOCK_ROWS, width), _F32), ((BLOCK_ROWS, width), _BF16)]
    return _sublayer_call(_mixer_a_kernel, "mixer_a_sublayer", x2d,
                          [gpre, gpost, win, lng, lnb, ws, bs_t, wout], scratch,
                          (n // BLOCK_ROWS,), lambda i: (i, 0), ("parallel",))


def _mixer_b_call(x2d, gpre, gpost, win, wgrp, scale, wout, seq):
    n, d = x2d.shape
    width = wout.shape[0]
    gdim = wgrp.shape[1]
    blocks = seq // BLOCK_ROWS
    n_sub = BLOCK_ROWS // SUB_ROWS
    scratch = [((BLOCK_ROWS + POOL_HALO, width), _F32),
               ((n_sub * (SUB_ROWS + POOL_HALO), gdim), _F32),
               ((n_sub * (SUB_ROWS + POOL_HALO), gdim), _F32),
               ((BLOCK_ROWS, width), _BF16)]
    return _sublayer_call(_mixer_b_kernel, "mixer_b_sublayer", x2d,
                          [gpre, gpost, win, wgrp, scale, wout], scratch,
                          (n // seq, blocks), lambda b, t: (b * blocks + t, 0),
                          ("parallel", "arbitrary"))


def kernel(x, a_w_in, a_ln_g, a_ln_b, a_w_s, a_b_s, a_w_out, b_w_in, b_w_grp, b_scale, b_w_out, mix_pre_g, mix_post_g, ffn_pre_g, ffn_post_g, ffn_w_gate, ffn_w_up, ffn_w_down):
    bsz, seq, d = x.shape
    depth = mix_pre_g.shape[0]
    assert seq % BLOCK_ROWS == 0 and BLOCK_ROWS % SUB_ROWS == 0
    assert SUB_ROWS % a_w_s.shape[-1] == 0 and SUB_ROWS % V7X_SUBLANES == 0
    assert len(B_WINDOWS) == b_w_grp.shape[1] and max(B_WINDOWS) <= POOL_HALO - V7X_SUBLANES
    assert (a_w_out.shape[1] // a_w_s.shape[1]) % V7X_MXU_DIM == 0

    x2d = x.reshape(bsz * seq, d)
    for i in range(depth):
        j = i // 2
        gpre = mix_pre_g[i][None, :]
        gpost = mix_post_g[i][None, :]
        if i % 2 == 0:
            x2d = _mixer_a_call(
                x2d, gpre, gpost, a_w_in[j].astype(_BF16), a_ln_g[j][None, :], a_ln_b[j][None, :],
                a_w_s[j], jnp.transpose(a_b_s[j]), a_w_out[j].astype(_BF16))
        else:
            x2d = _mixer_b_call(
                x2d, gpre, gpost, b_w_in[j].astype(_BF16), b_w_grp[j].astype(_BF16),
                b_scale[j][None, :], b_w_out[j].astype(_BF16), seq)
        x2d = _ffn_call(x2d, ffn_pre_g[i][None, :], ffn_post_g[i][None, :],
                        ffn_w_gate[i].astype(_BF16), ffn_w_up[i].astype(_BF16),
                        ffn_w_down[i].astype(_BF16))
    return x2d.reshape(bsz, seq, d)
```

```python
import functools

import jax
import jax.numpy as jnp
from jax import lax
from jax.experimental import pallas as pl
from jax.experimental.pallas import tpu as pltpu

EPS = 1e-6
B_WINDOWS = (2, 4, 8, 16)

V7X_VMEM_BYTES = 64 * 1024 * 1024
V7X_SUBLANES = 8
V7X_MXU_DIM = 256

BLOCK_ROWS = 512
POOL_HALO = 32

_F32 = jnp.float32
_BF16 = jnp.bfloat16


def _rms_scale(x, g):
    ms = jnp.mean(x * x, axis=-1, keepdims=True)
    return x * lax.rsqrt(ms + EPS) * g


def _gelu_exact(x):
    return 0.5 * x * (1.0 + lax.erf(x * (0.5 ** 0.5)))


def _dot(a, b):
    return jnp.dot(a, b, preferred_element_type=_F32)


def _col_chunks(n, chunk):
    assert n % chunk == 0
    return [slice(lo, lo + chunk) for lo in range(0, n, chunk)]


def _ffn_head(step, finish_previous, h_ref, s_ref, wg_ref, wu_ref):
    del step
    finish_previous()
    for csl in _col_chunks(s_ref.shape[1], V7X_MXU_DIM):
        g = _dot(h_ref[...], wg_ref[:, csl])
        u = _dot(h_ref[...], wu_ref[:, csl])
        s_ref[:, csl] = (g * jax.nn.sigmoid(g) * u).astype(_BF16)


def _mixer_a_head(step, finish_previous, h_ref, s_ref, win_ref, lng_ref, lnb_ref, ws_ref, bs_ref,
                  v_ref, u_ref, vn_ref):
    del step
    finish_previous()
    rows, width = s_ref.shape
    groups, t_chunk, _ = ws_ref.shape
    gdim = width // groups
    chunks = _col_chunks(width, 2 * V7X_MXU_DIM)

    rowsum = jnp.zeros((rows, 1), _F32)
    for csl in chunks:
        vc = _gelu_exact(_dot(h_ref[...], win_ref[:, width + csl.start:width + csl.stop]))
        v_ref[:, csl] = vc
        rowsum = rowsum + jnp.sum(vc, axis=-1, keepdims=True)
    mu = rowsum * (1.0 / width)

    for csl in chunks:
        u_ref[:, csl] = _gelu_exact(_dot(h_ref[...], win_ref[:, csl]))
    sq = jnp.zeros((rows, 1), _F32)
    for csl in chunks:
        xc = v_ref[:, csl] - mu
        sq = sq + jnp.sum(xc * xc, axis=-1, keepdims=True)
    rstd = lax.rsqrt(sq * (1.0 / width) + EPS)
    for csl in chunks:
        vn = (v_ref[:, csl] - mu) * rstd * lng_ref[:, csl] + lnb_ref[:, csl]
        vn_ref[:, csl] = vn.astype(_BF16)

    r_id = lax.broadcasted_iota(jnp.int32, (t_chunk, t_chunk), 0)
    c_id = lax.broadcasted_iota(jnp.int32, (t_chunk, t_chunk), 1)
    causal = r_id >= c_id
    for g in range(groups):
        gsl = slice(g * gdim, (g + 1) * gdim)
        w_tril = jnp.where(causal, ws_ref[g], 0.0).astype(_BF16)
        bias = bs_ref[:, g:g + 1]
        for n in range(rows // t_chunk):
            tsl = slice(n * t_chunk, (n + 1) * t_chunk)
            sv = _dot(w_tril, vn_ref[tsl, gsl]) + bias
            s_ref[tsl, gsl] = (u_ref[tsl, gsl] * sv).astype(_BF16)


def _mixer_b_init(p_ref, s0_ref, s1_ref):
    del s0_ref, s1_ref
    p_ref[0:POOL_HALO, :] = jnp.zeros((POOL_HALO, p_ref.shape[1]), _F32)


def _mixer_b_head(blocks_per_seq, step, finish_previous, h_ref, s_ref, win_ref, wgrp_ref, scale_ref,
                  p_ref, s0_ref, s1_ref):
    rows = h_ref.shape[0]
    groups, gdim, _ = wgrp_ref.shape
    halo = POOL_HALO
    t_idx = step % blocks_per_seq

    p_ref[pl.ds(halo, rows), :] = _dot(h_ref[...], win_ref[...])
    finish_previous()
    pos = t_idx * rows + lax.broadcasted_iota(jnp.int32, (rows, 1), 0)
    for g in range(groups):
        win = B_WINDOWS[g]
        gsl = slice(g * gdim, (g + 1) * gdim)
        src, dst = None, s0_ref
        shift, lo = 1, V7X_SUBLANES
        while shift < win:
            n = rows + halo - lo
            if src is None:
                cur = p_ref[pl.ds(lo, n), gsl]
                prev = p_ref[pl.ds(lo - shift, n), gsl]
            else:
                cur = src[pl.ds(lo, n), :]
                prev = src[pl.ds(lo - shift, n), :]
            dst[pl.ds(lo, n), :] = cur + prev
            src, dst = dst, (s1_ref if dst is s0_ref else s0_ref)
            shift *= 2
            lo += V7X_SUBLANES
        total = src[pl.ds(halo, rows), :]
        inv_count = 1.0 / jnp.minimum(pos + 1, win).astype(_F32)
        pooled = total * inv_count - p_ref[pl.ds(halo, rows), gsl]
        mixed = _dot(pooled.astype(_BF16), wgrp_ref[g]) * scale_ref[:, gsl]
        s_ref[:, gsl] = mixed.astype(_BF16)

    next_in_same_seq = (step + 1) % blocks_per_seq != 0
    p_ref[0:halo, :] = jnp.where(next_in_same_seq, p_ref[rows:rows + halo, :], 0.0)


def _sublayer_kernel(head, init, n_blocks, n_head_in,
                     x_prev_ref, x_cur_ref, gpre_ref, gpost_ref, wlast_ref, *refs):
    head_in = refs[:n_head_in]
    o_ref, h_ref, s_ref = refs[n_head_in:n_head_in + 3]
    head_scratch = refs[n_head_in + 3:]
    step = pl.program_id(0)

    def finish_previous():
        f = _dot(s_ref[...], wlast_ref[...])
        o_ref[...] = x_prev_ref[...] + _rms_scale(f, gpost_ref[...])

    def start_current(finish):
        h_ref[...] = _rms_scale(x_cur_ref[...], gpre_ref[...]).astype(_BF16)
        head(step, finish, h_ref, s_ref, *head_in, *head_scratch)

    @pl.when(step == 0)
    def _():
        if init is not None:
            init(*head_scratch)
        start_current(lambda: None)

    @pl.when(jnp.logical_and(step > 0, step < n_blocks))
    def _():
        start_current(finish_previous)

    @pl.when(step == n_blocks)
    def _():
        finish_previous()


def _resident(shape):
    zeros = (0,) * len(shape)
    return pl.BlockSpec(shape, lambda i: zeros, pipeline_mode=pl.Buffered(1))


def _nbytes(shape, dtype):
    n = 1
    for s in shape:
        n *= s
    return n * jnp.dtype(dtype).itemsize


def _sublayer_call(head, init, name, x2d, gpre, gpost, wlast, head_in, head_scratch):
    n, d = x2d.shape
    rows = BLOCK_ROWS
    n_blocks = n // rows
    last = n_blocks - 1
    residents = [gpre, gpost, wlast] + list(head_in)
    scratch = [((rows, d), _BF16), ((rows, wlast.shape[0]), _BF16)] + list(head_scratch)

    io = 3 * 2 * _nbytes((rows, d), _F32)
    temps = 4 * _nbytes((rows, 2 * V7X_MXU_DIM), _F32) + 2 * _nbytes((rows, d), _F32)
    vmem = (sum(_nbytes(r.shape, r.dtype) for r in residents)
            + sum(_nbytes(s, t) for s, t in scratch) + io + temps)
    assert vmem <= V7X_VMEM_BYTES, vmem

    body = functools.partial(_sublayer_kernel, head, init, n_blocks, len(head_in))
    return pl.pallas_call(
        body,
        grid=(n_blocks + 1,),
        in_specs=[pl.BlockSpec((rows, d), lambda i: (jnp.maximum(i - 1, 0), 0)),
                  pl.BlockSpec((rows, d), lambda i: (jnp.minimum(i, last), 0))]
                 + [_resident(r.shape) for r in residents],
        out_specs=pl.BlockSpec((rows, d), lambda i: (jnp.maximum(i - 1, 0), 0)),
        out_shape=jax.ShapeDtypeStruct((n, d), x2d.dtype),
        scratch_shapes=[pltpu.VMEM(s, t) for s, t in scratch],
        compiler_params=pltpu.CompilerParams(
            dimension_semantics=("arbitrary",), vmem_limit_bytes=vmem),
        name=name,
    )(x2d, x2d, *residents)


def kernel(x, a_w_in, a_ln_g, a_ln_b, a_w_s, a_b_s, a_w_out, b_w_in, b_w_grp, b_scale, b_w_out, mix_pre_g, mix_post_g, ffn_pre_g, ffn_post_g, ffn_w_gate, ffn_w_up, ffn_w_down):
    bsz, seq, d = x.shape
    depth = mix_pre_g.shape[0]
    rows = BLOCK_ROWS
    a_width = a_w_out.shape[1]
    b_width = b_w_out.shape[1]
    b_gdim = b_w_grp.shape[2]
    assert seq % rows == 0 and rows % a_w_s.shape[-1] == 0 and rows % V7X_SUBLANES == 0
    assert len(B_WINDOWS) == b_w_grp.shape[1] and max(B_WINDOWS) <= POOL_HALO - V7X_SUBLANES
    assert (a_width // a_w_s.shape[1]) % V7X_MXU_DIM == 0

    x2d = x.reshape(bsz * seq, d)
    for i in range(depth):
        j = i // 2
        gpre = mix_pre_g[i][None, :]
        gpost = mix_post_g[i][None, :]
        if i % 2 == 0:
            x2d = _sublayer_call(
                _mixer_a_head, None, "mixer_a_sublayer", x2d, gpre, gpost,
                a_w_out[j].astype(_BF16),
                [a_w_in[j].astype(_BF16), a_ln_g[j][None, :], a_ln_b[j][None, :],
                 a_w_s[j], jnp.transpose(a_b_s[j])],
                [((rows, a_width), _F32), ((rows, a_width), _F32), ((rows, a_width), _BF16)])
        else:
            x2d = _sublayer_call(
                functools.partial(_mixer_b_head, seq // rows), _mixer_b_init,
                "mixer_b_sublayer", x2d, gpre, gpost, b_w_out[j].astype(_BF16),
                [b_w_in[j].astype(_BF16), b_w_grp[j].astype(_BF16), b_scale[j][None, :]],
                [((rows + POOL_HALO, b_width), _F32), ((rows + POOL_HALO, b_gdim), _F32),
                 ((rows + POOL_HALO, b_gdim), _F32)])
        x2d = _sublayer_call(
            _ffn_head, None, "ffn_sublayer", x2d, ffn_pre_g[i][None, :], ffn_post_g[i][None, :],
            ffn_w_down[i].astype(_BF16),
            [ffn_w_gate[i].astype(_BF16), ffn_w_up[i].astype(_BF16)], [])
    return x2d.reshape(bsz, seq, d)
```

```python
import functools

import jax
import jax.numpy as jnp
from jax import lax
from jax.experimental import pallas as pl
from jax.experimental.pallas import tpu as pltpu

EPS = 1e-6
B_WINDOWS = (2, 4, 8, 16)

V7X_VMEM_BYTES = 64 * 1024 * 1024
V7X_LANES = 128
V7X_SUBLANES = 8
V7X_BF16_SUBLANES = 16
V7X_MXU_DIM = 256

BLOCK_ROWS = 1024
MIXER_SUB_ROWS = 512
FFN_SUB_ROWS = 256
POOL_HALO = 32

_F32 = jnp.float32
_BF16 = jnp.bfloat16


def _rms_scale(x, g):
    ms = jnp.mean(x * x, axis=-1, keepdims=True)
    return x * lax.rsqrt(ms + EPS) * g


def _gelu_exact(x):
    return 0.5 * x * (1.0 + lax.erf(x * (0.5 ** 0.5)))


def _dot(a, b):
    return jnp.dot(a, b, preferred_element_type=_F32)


def _col_chunks(n, chunk):
    assert n % chunk == 0
    return [slice(lo, lo + chunk) for lo in range(0, n, chunk)]


def _sub_tiles(rows, sub):
    return [slice(lo, lo + sub) for lo in range(0, rows, sub)]


def _ffn_kernel(x_ref, gpre_ref, gpost_ref, wg_ref, wu_ref, wd_ref, o_ref, h_ref, a_ref):
    for rsl in _sub_tiles(x_ref.shape[0], FFN_SUB_ROWS):
        h_ref[rsl, :] = _rms_scale(x_ref[rsl, :], gpre_ref[...]).astype(_BF16)
        for csl in _col_chunks(wd_ref.shape[0], V7X_MXU_DIM):
            g = _dot(h_ref[rsl, :], wg_ref[:, csl])
            u = _dot(h_ref[rsl, :], wu_ref[:, csl])
            a_ref[rsl, csl] = (g * jax.nn.sigmoid(g) * u).astype(_BF16)
        f = _dot(a_ref[rsl, :], wd_ref[...])
        o_ref[rsl, :] = x_ref[rsl, :] + _rms_scale(f, gpost_ref[...])


def _mixer_a_kernel(x_ref, gpre_ref, gpost_ref, win_ref, lng_ref, lnb_ref,
                    ws_ref, bs_ref, wout_ref, o_ref, h_ref, v_ref, gated_ref):
    sub = MIXER_SUB_ROWS
    width = wout_ref.shape[0]
    groups, t_chunk, _ = ws_ref.shape
    gdim = width // groups
    v_chunks = _col_chunks(width, 2 * V7X_MXU_DIM)

    r_id = lax.broadcasted_iota(jnp.int32, (t_chunk, t_chunk), 0)
    c_id = lax.broadcasted_iota(jnp.int32, (t_chunk, t_chunk), 1)
    causal = r_id >= c_id

    for rsl in _sub_tiles(x_ref.shape[0], sub):
        h_ref[rsl, :] = _rms_scale(x_ref[rsl, :], gpre_ref[...]).astype(_BF16)

        rowsum = jnp.zeros((sub, 1), _F32)
        for csl in v_chunks:
            wsl = slice(width + csl.start, width + csl.stop)
            vc = _gelu_exact(_dot(h_ref[rsl, :], win_ref[:, wsl]))
            v_ref[rsl, csl] = vc
            rowsum = rowsum + jnp.sum(vc, axis=-1, keepdims=True)
        mu = rowsum * (1.0 / width)
        sq = jnp.zeros((sub, 1), _F32)
        for csl in v_chunks:
            xc = v_ref[rsl, csl] - mu
            sq = sq + jnp.sum(xc * xc, axis=-1, keepdims=True)
        rstd = lax.rsqrt(sq * (1.0 / width) + EPS)

        for g in range(groups):
            gsl = slice(g * gdim, (g + 1) * gdim)
            w_tril = jnp.where(causal, ws_ref[g], 0.0).astype(_BF16)
            bias = bs_ref[:, g:g + 1]
            vn = ((v_ref[rsl, gsl] - mu) * rstd * lng_ref[:, gsl] + lnb_ref[:, gsl]).astype(_BF16)
            ug = _gelu_exact(_dot(h_ref[rsl, :], win_ref[:, gsl]))
            for n in range(sub // t_chunk):
                tsl = slice(n * t_chunk, (n + 1) * t_chunk)
                sv = _dot(w_tril, vn[tsl, :]) + bias
                osl = slice(rsl.start + tsl.start, rsl.start + tsl.stop)
                gated_ref[osl, gsl] = (ug[tsl, :] * sv).astype(_BF16)

        m = _dot(gated_ref[rsl, :], wout_ref[...])
        o_ref[rsl, :] = x_ref[rsl, :] + _rms_scale(m, gpost_ref[...])


def _mixer_b_kernel(x_ref, gpre_ref, gpost_ref, win_ref, wgrp_ref, scale_ref, wout_ref,
                    o_ref, p_ref, s0_ref, s1_ref, mixed_ref):
    sub = MIXER_SUB_ROWS
    rows = x_ref.shape[0]
    gdim = wgrp_ref.shape[1]
    groups = wgrp_ref.shape[0] // gdim
    halo = p_ref.shape[0] - rows
    t_idx = pl.program_id(1)

    @pl.when(t_idx == 0)
    def _():
        p_ref[0:halo, :] = jnp.zeros((halo, p_ref.shape[1]), _F32)

    for rsl in _sub_tiles(rows, sub):
        h = _rms_scale(x_ref[rsl, :], gpre_ref[...]).astype(_BF16)
        p_ref[pl.ds(halo + rsl.start, sub), :] = _dot(h, win_ref[...])

    for s_idx, rsl in enumerate(_sub_tiles(rows, sub)):
        base = rsl.start
        pos = t_idx * rows + base + lax.broadcasted_iota(jnp.int32, (sub, 1), 0)
        s_lo = s_idx * (sub + halo)
        for g in range(groups):
            win = B_WINDOWS[g]
            gsl = slice(g * gdim, (g + 1) * gdim)
            src, dst = None, s0_ref
            shift, lo = 1, V7X_SUBLANES
            while shift < win:
                n = sub + halo - lo
                if src is None:
                    cur = p_ref[pl.ds(base + lo, n), gsl]
                    prev = p_ref[pl.ds(base + lo - shift, n), gsl]
                else:
                    cur = src[pl.ds(s_lo + lo, n), :]
                    prev = src[pl.ds(s_lo + lo - shift, n), :]
                dst[pl.ds(s_lo + lo, n), :] = cur + prev
                src, dst = dst, (s1_ref if dst is s0_ref else s0_ref)
                shift *= 2
                lo += V7X_SUBLANES
            total = src[pl.ds(s_lo + halo, sub), :]
            inv_count = 1.0 / jnp.minimum(pos + 1, win).astype(_F32)
            pooled = total * inv_count - p_ref[pl.ds(base + halo, sub), gsl]
            mixed = _dot(pooled.astype(_BF16), wgrp_ref[gsl, :]) * scale_ref[:, gsl]
            mixed_ref[rsl, gsl] = mixed.astype(_BF16)
        m = _dot(mixed_ref[rsl, :], wout_ref[...])
        o_ref[rsl, :] = x_ref[rsl, :] + _rms_scale(m, gpost_ref[...])

    p_ref[0:halo, :] = p_ref[rows:rows + halo, :]


def _convert_then(body, n_in, n_conv, *refs):
    src = refs[n_in:n_in + n_conv]
    o_ref = refs[n_in + n_conv]
    dst = refs[n_in + n_conv + 1:n_in + 2 * n_conv + 1]
    scratch = refs[n_in + 2 * n_conv + 1:]
    for s_ref, d_ref in zip(src, dst):
        d_ref[...] = s_ref[...].astype(_BF16)
    body(*refs[:n_in], o_ref, *scratch)


def _resident(shape):
    zeros = (0,) * len(shape)
    return pl.BlockSpec(shape, lambda *_: zeros, pipeline_mode=pl.Buffered(1))


def _step_slicing(shape, n_steps):
    rows, cols = shape
    for col_parts in range(1, n_steps + 1):
        row_parts, rem = divmod(n_steps, col_parts)
        if (rem == 0 and rows % (row_parts * V7X_BF16_SUBLANES) == 0
                and cols % (col_parts * V7X_LANES) == 0):
            return (rows // row_parts, cols // col_parts), col_parts
    raise ValueError(f"no tile-aligned {n_steps}-way split of {shape}")


def _nbytes(shape, dtype):
    n = 1
    for s in shape:
        n *= s
    return n * jnp.dtype(dtype).itemsize


def _sublayer_call(body, name, sub_rows, x2d, residents, scratch, to_convert, seq=None):
    n, d = x2d.shape
    rows = BLOCK_ROWS
    n_steps = n // rows
    if seq is None:
        grid, semantics = (n_steps,), ("parallel",)
        flat = lambda i: i
    else:
        blocks = seq // rows
        grid, semantics = (n // seq, blocks), ("parallel", "arbitrary")
        flat = lambda b, t: b * blocks + t
    row_spec = pl.BlockSpec((rows, d), lambda *idx: (flat(*idx), 0))

    def conv_spec(w):
        block, col_parts = _step_slicing(w.shape, n_steps)
        return pl.BlockSpec(block, lambda *idx: (flat(*idx) // col_parts, flat(*idx) % col_parts))

    conv_specs = [conv_spec(w) for w in to_convert]

    io = 2 * 2 * _nbytes((rows, d), _F32)
    conv = sum(2 * (_nbytes(s.block_shape, _F32) + _nbytes(s.block_shape, _BF16)) for s in conv_specs)
    temps = (rows // sub_rows) * (4 * _nbytes((sub_rows, 2 * V7X_MXU_DIM), _F32)
                                  + 2 * _nbytes((sub_rows, d), _F32))
    vmem = (sum(_nbytes(r.shape, r.dtype) for r in residents)
            + sum(_nbytes(s, t) for s, t in scratch) + io + conv + temps)
    assert vmem <= V7X_VMEM_BYTES, vmem

    outs = pl.pallas_call(
        functools.partial(_convert_then, body, 1 + len(residents), len(to_convert)),
        grid=grid,
        in_specs=[row_spec] + [_resident(r.shape) for r in residents] + conv_specs,
        out_specs=[row_spec] + [conv_spec(w) for w in to_convert],
        out_shape=[jax.ShapeDtypeStruct((n, d), x2d.dtype)]
                  + [jax.ShapeDtypeStruct(w.shape, _BF16) for w in to_convert],
        scratch_shapes=[pltpu.VMEM(s, t) for s, t in scratch],
        compiler_params=pltpu.CompilerParams(
            dimension_semantics=semantics, vmem_limit_bytes=vmem),
        name=name,
    )(x2d, *residents, *to_convert)
    return outs[0], list(outs[1:])


def kernel(x, a_w_in, a_ln_g, a_ln_b, a_w_s, a_b_s, a_w_out, b_w_in, b_w_grp, b_scale, b_w_out, mix_pre_g, mix_post_g, ffn_pre_g, ffn_post_g, ffn_w_gate, ffn_w_up, ffn_w_down):
    bsz, seq, d = x.shape
    depth = mix_pre_g.shape[0]
    rows = BLOCK_ROWS
    a_width = a_w_out.shape[1]
    b_width = b_w_out.shape[1]
    b_gdim = b_w_grp.shape[2]
    d_ff = ffn_w_down.shape[1]
    assert seq % rows == 0 and rows % MIXER_SUB_ROWS == 0 and rows % FFN_SUB_ROWS == 0
    assert MIXER_SUB_ROWS % a_w_s.shape[-1] == 0
    assert len(B_WINDOWS) == b_w_grp.shape[1] and max(B_WINDOWS) <= POOL_HALO - V7X_SUBLANES
    assert (a_width // a_w_s.shape[1]) % V7X_MXU_DIM == 0
    n_sub_b = rows // MIXER_SUB_ROWS

    def mixer_a(i, j):
        smalls = [mix_pre_g[i][None, :], mix_post_g[i][None, :]]
        mids = [a_ln_g[j][None, :], a_ln_b[j][None, :], a_w_s[j], jnp.transpose(a_b_s[j])]
        scratch = [((rows, d), _BF16), ((rows, a_width), _F32), ((rows, a_width), _BF16)]
        def call(x2d, w, nxt):
            return _sublayer_call(_mixer_a_kernel, "mixer_a_sublayer", MIXER_SUB_ROWS, x2d,
                                  smalls + [w[0]] + mids + [w[1]], scratch, nxt)
        return [a_w_in[j], a_w_out[j]], call

    def mixer_b(i, j):
        smalls = [mix_pre_g[i][None, :], mix_post_g[i][None, :]]
        scratch = [((rows + POOL_HALO, b_width), _F32),
                   ((n_sub_b * (MIXER_SUB_ROWS + POOL_HALO), b_gdim), _F32),
                   ((n_sub_b * (MIXER_SUB_ROWS + POOL_HALO), b_gdim), _F32),
                   ((rows, b_width), _BF16)]
        def call(x2d, w, nxt):
            return _sublayer_call(_mixer_b_kernel, "mixer_b_sublayer", MIXER_SUB_ROWS, x2d,
                                  smalls + [w[0], w[1], b_scale[j][None, :], w[2]], scratch, nxt,
                                  seq=seq)
        return [b_w_in[j], b_w_grp[j].reshape(b_width, b_gdim), b_w_out[j]], call

    def ffn(i):
        smalls = [ffn_pre_g[i][None, :], ffn_post_g[i][None, :]]
        scratch = [((rows, d), _BF16), ((rows, d_ff), _BF16)]
        def call(x2d, w, nxt):
            return _sublayer_call(_ffn_kernel, "ffn_sublayer", FFN_SUB_ROWS, x2d,
                                  smalls + list(w), scratch, nxt)
        return [ffn_w_gate[i], ffn_w_up[i], ffn_w_down[i]], call

    sublayers = []
    for i in range(depth):
        sublayers.append(mixer_a(i, i // 2) if i % 2 == 0 else mixer_b(i, i // 2))
        sublayers.append(ffn(i))

    x2d = x.reshape(bsz * seq, d)
    weights = [w.astype(_BF16) for w in sublayers[0][0]]
    for k, (_, call) in enumerate(sublayers):
        nxt = sublayers[k + 1][0] if k + 1 < len(sublayers) else []
        x2d, weights = call(x2d, weights, nxt)
    return x2d.reshape(bsz, seq, d)
```

```python
import functools

import jax
import jax.numpy as jnp
from jax import lax
from jax.experimental import pallas as pl
from jax.experimental.pallas import tpu as pltpu

EPS = 1e-6
B_WINDOWS = (2, 4, 8, 16)

V7X_VMEM_BYTES = 64 * 1024 * 1024
V7X_LANES = 128
V7X_SUBLANES = 8
V7X_BF16_SUBLANES = 16
V7X_MXU_DIM = 256

BLOCK_ROWS = 1024
MIXER_SUB_ROWS = 512
FFN_SUB_ROWS = 256
POOL_HALO = 32

_F32 = jnp.float32
_BF16 = jnp.bfloat16


def _rms_scale(x, g):
    ms = jnp.mean(x * x, axis=-1, keepdims=True)
    return x * lax.rsqrt(ms + EPS) * g


def _gelu_exact(x):
    return 0.5 * x * (1.0 + lax.erf(x * (0.5 ** 0.5)))


def _dot(a, b):
    return jnp.dot(a, b, preferred_element_type=_F32)


def _col_chunks(n, chunk):
    assert n % chunk == 0
    return [slice(lo, lo + chunk) for lo in range(0, n, chunk)]


def _sub_tiles(rows, sub):
    return [slice(lo, lo + sub) for lo in range(0, rows, sub)]


def _ffn_kernel(x_ref, gpre_ref, gpost_ref, wg_ref, wu_ref, wd_ref, o_ref, h_ref, a_ref):
    for rsl in _sub_tiles(x_ref.shape[0], FFN_SUB_ROWS):
        h_ref[rsl, :] = _rms_scale(x_ref[rsl, :], gpre_ref[...]).astype(_BF16)
        for csl in _col_chunks(wd_ref.shape[0], V7X_MXU_DIM):
            g = _dot(h_ref[rsl, :], wg_ref[:, csl])
            u = _dot(h_ref[rsl, :], wu_ref[:, csl])
            a_ref[rsl, csl] = (g * jax.nn.sigmoid(g) * u).astype(_BF16)
        f = _dot(a_ref[rsl, :], wd_ref[...])
        o_ref[rsl, :] = x_ref[rsl, :] + _rms_scale(f, gpost_ref[...])


def _mixer_a_kernel(x_ref, gpre_ref, gpost_ref, win_ref, lng_ref, lnb_ref,
                    ws_ref, bs_ref, wout_ref, o_ref, h_ref, v_ref, gated_ref):
    sub = MIXER_SUB_ROWS
    width = wout_ref.shape[0]
    groups, t_chunk, _ = ws_ref.shape
    gdim = width // groups
    v_chunks = _col_chunks(width, 2 * V7X_MXU_DIM)

    r_id = lax.broadcasted_iota(jnp.int32, (t_chunk, t_chunk), 0)
    c_id = lax.broadcasted_iota(jnp.int32, (t_chunk, t_chunk), 1)
    causal = r_id >= c_id

    for rsl in _sub_tiles(x_ref.shape[0], sub):
        h_ref[rsl, :] = _rms_scale(x_ref[rsl, :], gpre_ref[...]).astype(_BF16)

        rowsum = jnp.zeros((sub, 1), _F32)
        for csl in v_chunks:
            wsl = slice(width + csl.start, width + csl.stop)
            vc = _gelu_exact(_dot(h_ref[rsl, :], win_ref[:, wsl]))
            v_ref[rsl, csl] = vc
            rowsum = rowsum + jnp.sum(vc, axis=-1, keepdims=True)
        mu = rowsum * (1.0 / width)
        sq = jnp.zeros((sub, 1), _F32)
        for csl in v_chunks:
            xc = v_ref[rsl, csl] - mu
            sq = sq + jnp.sum(xc * xc, axis=-1, keepdims=True)
        rstd = lax.rsqrt(sq * (1.0 / width) + EPS)

        for g in range(groups):
            gsl = slice(g * gdim, (g + 1) * gdim)
            w_tril = jnp.where(causal, ws_ref[g], 0.0).astype(_BF16)
            bias = bs_ref[:, g:g + 1]
            vn = ((v_ref[rsl, gsl] - mu) * rstd * lng_ref[:, gsl] + lnb_ref[:, gsl]).astype(_BF16)
            ug = _gelu_exact(_dot(h_ref[rsl, :], win_ref[:, gsl]))
            for n in range(sub // t_chunk):
                tsl = slice(n * t_chunk, (n + 1) * t_chunk)
                sv = _dot(w_tril, vn[tsl, :]) + bias
                osl = slice(rsl.start + tsl.start, rsl.start + tsl.stop)
                gated_ref[osl, gsl] = (ug[tsl, :] * sv).astype(_BF16)

        m = _dot(gated_ref[rsl, :], wout_ref[...])
        o_ref[rsl, :] = x_ref[rsl, :] + _rms_scale(m, gpost_ref[...])


def _mixer_b_kernel(x_ref, gpre_ref, gpost_ref, win_ref, wgrp_ref, scale_ref, wout_ref,
                    o_ref, p_ref, s0_ref, s1_ref, mixed_ref):
    sub = MIXER_SUB_ROWS
    rows = x_ref.shape[0]
    gdim = wgrp_ref.shape[1]
    groups = wgrp_ref.shape[0] // gdim
    halo = p_ref.shape[0] - rows
    t_idx = pl.program_id(1)

    @pl.when(t_idx == 0)
    def _():
        p_ref[0:halo, :] = jnp.zeros((halo, p_ref.shape[1]), _F32)

    for rsl in _sub_tiles(rows, sub):
        h = _rms_scale(x_ref[rsl, :], gpre_ref[...]).astype(_BF16)
        p_ref[pl.ds(halo + rsl.start, sub), :] = _dot(h, win_ref[...])

    for s_idx, rsl in enumerate(_sub_tiles(rows, sub)):
        base = rsl.start
        pos = t_idx * rows + base + lax.broadcasted_iota(jnp.int32, (sub, 1), 0)
        s_lo = s_idx * (sub + halo)
        for g in range(groups):
            win = B_WINDOWS[g]
            gsl = slice(g * gdim, (g + 1) * gdim)
            src, dst = None, s0_ref
            shift, lo = 1, V7X_SUBLANES
            while shift < win:
                n = sub + halo - lo
                if src is None:
                    cur = p_ref[pl.ds(base + lo, n), gsl]
                    prev = p_ref[pl.ds(base + lo - shift, n), gsl]
                else:
                    cur = src[pl.ds(s_lo + lo, n), :]
                    prev = src[pl.ds(s_lo + lo - shift, n), :]
                dst[pl.ds(s_lo + lo, n), :] = cur + prev
                src, dst = dst, (s1_ref if dst is s0_ref else s0_ref)
                shift *= 2
                lo += V7X_SUBLANES
            total = src[pl.ds(s_lo + halo, sub), :]
            inv_count = 1.0 / jnp.minimum(pos + 1, win).astype(_F32)
            pooled = total * inv_count - p_ref[pl.ds(base + halo, sub), gsl]
            mixed = _dot(pooled.astype(_BF16), wgrp_ref[gsl, :]) * scale_ref[:, gsl]
            mixed_ref[rsl, gsl] = mixed.astype(_BF16)
        m = _dot(mixed_ref[rsl, :], wout_ref[...])
        o_ref[rsl, :] = x_ref[rsl, :] + _rms_scale(m, gpost_ref[...])

    p_ref[0:halo, :] = p_ref[rows:rows + halo, :]


def _convert_then(body, n_in, n_conv, *refs):
    src = refs[n_in:n_in + n_conv]
    o_ref = refs[n_in + n_conv]
    dst = refs[n_in + n_conv + 1:n_in + 2 * n_conv + 1]
    scratch = refs[n_in + 2 * n_conv + 1:]
    for s_ref, d_ref in zip(src, dst):
        d_ref[...] = s_ref[...].astype(_BF16)
    body(*refs[:n_in], o_ref, *scratch)


def _resident(shape):
    zeros = (0,) * len(shape)
    return pl.BlockSpec(shape, lambda *_: zeros, pipeline_mode=pl.Buffered(1))


def _step_slicing(shape, n_steps):
    rows, cols = shape
    for col_parts in range(1, n_steps + 1):
        row_parts, rem = divmod(n_steps, col_parts)
        if (rem == 0 and rows % (row_parts * V7X_BF16_SUBLANES) == 0
                and cols % (col_parts * V7X_LANES) == 0):
            return (rows // row_parts, cols // col_parts), col_parts
    raise ValueError(f"no tile-aligned {n_steps}-way split of {shape}")


def _nbytes(shape, dtype):
    n = 1
    for s in shape:
        n *= s
    return n * jnp.dtype(dtype).itemsize


def _sublayer_call(body, name, sub_rows, x2d, residents, scratch, to_convert, seq=None):
    n, d = x2d.shape
    rows = BLOCK_ROWS
    n_steps = n // rows
    if seq is None:
        grid, semantics = (n_steps,), ("parallel",)
        flat = lambda i: i
    else:
        blocks = seq // rows
        grid, semantics = (n // seq, blocks), ("parallel", "arbitrary")
        flat = lambda b, t: b * blocks + t
    row_spec = pl.BlockSpec((rows, d), lambda *idx: (flat(*idx), 0))

    def conv_spec(stacked, layer, squeezed_layer_axis):
        block, col_parts = _step_slicing(stacked.shape[1:], n_steps)
        if squeezed_layer_axis:
            return pl.BlockSpec((None,) + block, lambda *idx: (
                layer, flat(*idx) // col_parts, flat(*idx) % col_parts))
        return pl.BlockSpec(block, lambda *idx: (flat(*idx) // col_parts, flat(*idx) % col_parts))

    conv_specs = [conv_spec(w, layer, True) for w, layer in to_convert]

    io = 2 * 2 * _nbytes((rows, d), _F32)
    conv = sum(2 * (_nbytes(s.block_shape[1:], _F32) + _nbytes(s.block_shape[1:], _BF16))
               for s in conv_specs)
    temps = (rows // sub_rows) * (4 * _nbytes((sub_rows, 2 * V7X_MXU_DIM), _F32)
                                  + 2 * _nbytes((sub_rows, d), _F32))
    vmem = (sum(_nbytes(r.shape, r.dtype) for r in residents)
            + sum(_nbytes(s, t) for s, t in scratch) + io + conv + temps)
    assert vmem <= V7X_VMEM_BYTES, vmem

    outs = pl.pallas_call(
        functools.partial(_convert_then, body, 1 + len(residents), len(to_convert)),
        grid=grid,
        in_specs=[row_spec] + [_resident(r.shape) for r in residents] + conv_specs,
        out_specs=[row_spec] + [conv_spec(w, layer, False) for w, layer in to_convert],
        out_shape=[jax.ShapeDtypeStruct((n, d), x2d.dtype)]
                  + [jax.ShapeDtypeStruct(w.shape[1:], _BF16) for w, _ in to_convert],
        scratch_shapes=[pltpu.VMEM(s, t) for s, t in scratch],
        compiler_params=pltpu.CompilerParams(
            dimension_semantics=semantics, vmem_limit_bytes=vmem),
        name=name,
    )(x2d, *residents, *[w for w, _ in to_convert])
    return outs[0], list(outs[1:])


def kernel(x, a_w_in, a_ln_g, a_ln_b, a_w_s, a_b_s, a_w_out, b_w_in, b_w_grp, b_scale, b_w_out, mix_pre_g, mix_post_g, ffn_pre_g, ffn_post_g, ffn_w_gate, ffn_w_up, ffn_w_down):
    bsz, seq, d = x.shape
    depth = mix_pre_g.shape[0]
    rows = BLOCK_ROWS
    a_width = a_w_out.shape[1]
    b_width = b_w_out.shape[1]
    b_gdim = b_w_grp.shape[2]
    d_ff = ffn_w_down.shape[1]
    assert seq % rows == 0 and rows % MIXER_SUB_ROWS == 0 and rows % FFN_SUB_ROWS == 0
    assert MIXER_SUB_ROWS % a_w_s.shape[-1] == 0
    assert len(B_WINDOWS) == b_w_grp.shape[1] and max(B_WINDOWS) <= POOL_HALO - V7X_SUBLANES
    assert (a_width // a_w_s.shape[1]) % V7X_MXU_DIM == 0
    n_sub_b = rows // MIXER_SUB_ROWS
    b_w_grp_rows = b_w_grp.reshape(b_w_grp.shape[0], b_width, b_gdim)

    def mixer_a(i, j):
        smalls = [mix_pre_g[i][None, :], mix_post_g[i][None, :]]
        mids = [a_ln_g[j][None, :], a_ln_b[j][None, :], a_w_s[j], jnp.transpose(a_b_s[j])]
        scratch = [((rows, d), _BF16), ((rows, a_width), _F32), ((rows, a_width), _BF16)]
        def call(x2d, w, nxt):
            return _sublayer_call(_mixer_a_kernel, "mixer_a_sublayer", MIXER_SUB_ROWS, x2d,
                                  smalls + [w[0]] + mids + [w[1]], scratch, nxt)
        return [(a_w_in, j), (a_w_out, j)], call

    def mixer_b(i, j):
        smalls = [mix_pre_g[i][None, :], mix_post_g[i][None, :]]
        scratch = [((rows + POOL_HALO, b_width), _F32),
                   ((n_sub_b * (MIXER_SUB_ROWS + POOL_HALO), b_gdim), _F32),
                   ((n_sub_b * (MIXER_SUB_ROWS + POOL_HALO), b_gdim), _F32),
                   ((rows, b_width), _BF16)]
        def call(x2d, w, nxt):
            return _sublayer_call(_mixer_b_kernel, "mixer_b_sublayer", MIXER_SUB_ROWS, x2d,
                                  smalls + [w[0], w[1], b_scale[j][None, :], w[2]], scratch, nxt,
                                  seq=seq)
        return [(b_w_in, j), (b_w_grp_rows, j), (b_w_out, j)], call

    def ffn(i):
        smalls = [ffn_pre_g[i][None, :], ffn_post_g[i][None, :]]
        scratch = [((rows, d), _BF16), ((rows, d_ff), _BF16)]
        def call(x2d, w, nxt):
            return _sublayer_call(_ffn_kernel, "ffn_sublayer", FFN_SUB_ROWS, x2d,
                                  smalls + list(w), scratch, nxt)
        return [(ffn_w_gate, i), (ffn_w_up, i), (ffn_w_down, i)], call

    sublayers = []
    for i in range(depth):
        sublayers.append(mixer_a(i, i // 2) if i % 2 == 0 else mixer_b(i, i // 2))
        sublayers.append(ffn(i))

    x2d = x.reshape(bsz * seq, d)
    weights = [w[layer].astype(_BF16) for w, layer in sublayers[0][0]]
    for k, (_, call) in enumerate(sublayers):
        nxt = sublayers[k + 1][0] if k + 1 < len(sublayers) else []
        x2d, weights = call(x2d, weights, nxt)
    return x2d.reshape(bsz, seq, d)
```

```python
import functools

import jax
import jax.numpy as jnp
from jax import lax
from jax.experimental import pallas as pl
from jax.experimental.pallas import tpu as pltpu

EPS = 1e-6
B_WINDOWS = (2, 4, 8, 16)

V7X_VMEM_BYTES = 64 * 1024 * 1024
V7X_LANES = 128
V7X_SUBLANES = 8
V7X_BF16_SUBLANES = 16
V7X_MXU_DIM = 256

BLOCK_ROWS = 1024
SUB_ROWS = 256
POOL_SUB_ROWS = 512
POOL_HALO = 32

_F32 = jnp.float32
_BF16 = jnp.bfloat16


def _rms_scale(x, g):
    ms = jnp.mean(x * x, axis=-1, keepdims=True)
    return x * lax.rsqrt(ms + EPS) * g


def _gelu_exact(x):
    return 0.5 * x * (1.0 + lax.erf(x * (0.5 ** 0.5)))


def _dot(a, b):
    return jnp.dot(a, b, preferred_element_type=_F32)


def _col_chunks(n, chunk):
    assert n % chunk == 0
    return [slice(lo, lo + chunk) for lo in range(0, n, chunk)]


def _sub_tiles(rows, sub):
    return [slice(lo, lo + sub) for lo in range(0, rows, sub)]


def _skewed(tiles, first_stage, rest_stages):
    first_stage(tiles[0])
    for i, rsl in enumerate(tiles):
        if i + 1 < len(tiles):
            first_stage(tiles[i + 1])
        rest_stages(rsl)


def _ffn_kernel(x_ref, gpre_ref, gpost_ref, wg_ref, wu_ref, wd_ref, o_ref, h_ref, a_ref):
    def hidden(rsl):
        h_ref[rsl, :] = _rms_scale(x_ref[rsl, :], gpre_ref[...]).astype(_BF16)
        for csl in _col_chunks(wd_ref.shape[0], V7X_MXU_DIM):
            g = _dot(h_ref[rsl, :], wg_ref[:, csl])
            u = _dot(h_ref[rsl, :], wu_ref[:, csl])
            a_ref[rsl, csl] = (g * jax.nn.sigmoid(g) * u).astype(_BF16)

    def project_out(rsl):
        f = _dot(a_ref[rsl, :], wd_ref[...])
        o_ref[rsl, :] = x_ref[rsl, :] + _rms_scale(f, gpost_ref[...])

    _skewed(_sub_tiles(x_ref.shape[0], SUB_ROWS), hidden, project_out)


def _mixer_a_kernel(x_ref, gpre_ref, gpost_ref, win_ref, lng_ref, lnb_ref,
                    ws_ref, bs_ref, wout_ref, o_ref, h_ref, v_ref, gated_ref):
    sub = SUB_ROWS
    width = wout_ref.shape[0]
    groups, t_chunk, _ = ws_ref.shape
    gdim = width // groups
    v_chunks = _col_chunks(width, 2 * V7X_MXU_DIM)

    r_id = lax.broadcasted_iota(jnp.int32, (t_chunk, t_chunk), 0)
    c_id = lax.broadcasted_iota(jnp.int32, (t_chunk, t_chunk), 1)
    causal = r_id >= c_id
    row_mean = {}

    def project_v(rsl):
        h_ref[rsl, :] = _rms_scale(x_ref[rsl, :], gpre_ref[...]).astype(_BF16)
        rowsum = jnp.zeros((sub, 1), _F32)
        for csl in v_chunks:
            wsl = slice(width + csl.start, width + csl.stop)
            vc = _gelu_exact(_dot(h_ref[rsl, :], win_ref[:, wsl]))
            v_ref[rsl, csl] = vc
            rowsum = rowsum + jnp.sum(vc, axis=-1, keepdims=True)
        row_mean[rsl.start] = rowsum * (1.0 / width)

    def gate_and_project_out(rsl):
        mu = row_mean[rsl.start]
        sq = jnp.zeros((sub, 1), _F32)
        for csl in v_chunks:
            xc = v_ref[rsl, csl] - mu
            sq = sq + jnp.sum(xc * xc, axis=-1, keepdims=True)
        rstd = lax.rsqrt(sq * (1.0 / width) + EPS)

        for g in range(groups):
            gsl = slice(g * gdim, (g + 1) * gdim)
            w_tril = jnp.where(causal, ws_ref[g], 0.0).astype(_BF16)
            bias = bs_ref[:, g:g + 1]
            vn = ((v_ref[rsl, gsl] - mu) * rstd * lng_ref[:, gsl] + lnb_ref[:, gsl]).astype(_BF16)
            ug = _gelu_exact(_dot(h_ref[rsl, :], win_ref[:, gsl]))
            for n in range(sub // t_chunk):
                tsl = slice(n * t_chunk, (n + 1) * t_chunk)
                sv = _dot(w_tril, vn[tsl, :]) + bias
                osl = slice(rsl.start + tsl.start, rsl.start + tsl.stop)
                gated_ref[osl, gsl] = (ug[tsl, :] * sv).astype(_BF16)

        m = _dot(gated_ref[rsl, :], wout_ref[...])
        o_ref[rsl, :] = x_ref[rsl, :] + _rms_scale(m, gpost_ref[...])

    _skewed(_sub_tiles(x_ref.shape[0], sub), project_v, gate_and_project_out)


def _mixer_b_kernel(x_ref, gpre_ref, gpost_ref, win_ref, wgrp_ref, scale_ref, wout_ref,
                    o_ref, p_ref, s0_ref, s1_ref, mixed_ref):
    sub = POOL_SUB_ROWS
    rows = x_ref.shape[0]
    gdim = wgrp_ref.shape[1]
    groups = wgrp_ref.shape[0] // gdim
    halo = p_ref.shape[0] - rows
    t_idx = pl.program_id(1)

    @pl.when(t_idx == 0)
    def _():
        p_ref[0:halo, :] = jnp.zeros((halo, p_ref.shape[1]), _F32)

    for rsl in _sub_tiles(rows, sub):
        h = _rms_scale(x_ref[rsl, :], gpre_ref[...]).astype(_BF16)
        p_ref[pl.ds(halo + rsl.start, sub), :] = _dot(h, win_ref[...])

    for s_idx, rsl in enumerate(_sub_tiles(rows, sub)):
        base = rsl.start
        pos = t_idx * rows + base + lax.broadcasted_iota(jnp.int32, (sub, 1), 0)
        s_lo = s_idx * (sub + halo)
        for g in range(groups):
            win = B_WINDOWS[g]
            gsl = slice(g * gdim, (g + 1) * gdim)
            src, dst = None, s0_ref
            shift, lo = 1, V7X_SUBLANES
            while shift < win:
                n = sub + halo - lo
                if src is None:
                    cur = p_ref[pl.ds(base + lo, n), gsl]
                    prev = p_ref[pl.ds(base + lo - shift, n), gsl]
                else:
                    cur = src[pl.ds(s_lo + lo, n), :]
                    prev = src[pl.ds(s_lo + lo - shift, n), :]
                dst[pl.ds(s_lo + lo, n), :] = cur + prev
                src, dst = dst, (s1_ref if dst is s0_ref else s0_ref)
                shift *= 2
                lo += V7X_SUBLANES
            total = src[pl.ds(s_lo + halo, sub), :]
            inv_count = 1.0 / jnp.minimum(pos + 1, win).astype(_F32)
            pooled = total * inv_count - p_ref[pl.ds(base + halo, sub), gsl]
            mixed = _dot(pooled.astype(_BF16), wgrp_ref[gsl, :]) * scale_ref[:, gsl]
            mixed_ref[rsl, gsl] = mixed.astype(_BF16)
        m = _dot(mixed_ref[rsl, :], wout_ref[...])
        o_ref[rsl, :] = x_ref[rsl, :] + _rms_scale(m, gpost_ref[...])

    p_ref[0:halo, :] = p_ref[rows:rows + halo, :]


def _convert_then(body, n_in, n_conv, *refs):
    src = refs[n_in:n_in + n_conv]
    o_ref = refs[n_in + n_conv]
    dst = refs[n_in + n_conv + 1:n_in + 2 * n_conv + 1]
    scratch = refs[n_in + 2 * n_conv + 1:]
    for s_ref, d_ref in zip(src, dst):
        d_ref[...] = s_ref[...].astype(_BF16)
    body(*refs[:n_in], o_ref, *scratch)


def _resident(shape):
    zeros = (0,) * len(shape)
    return pl.BlockSpec(shape, lambda *_: zeros, pipeline_mode=pl.Buffered(1))


def _step_slicing(shape, n_steps):
    rows, cols = shape
    for col_parts in range(1, n_steps + 1):
        row_parts, rem = divmod(n_steps, col_parts)
        if (rem == 0 and rows % (row_parts * V7X_BF16_SUBLANES) == 0
                and cols % (col_parts * V7X_LANES) == 0):
            return (rows // row_parts, cols // col_parts), col_parts
    raise ValueError(f"no tile-aligned {n_steps}-way split of {shape}")


def _nbytes(shape, dtype):
    n = 1
    for s in shape:
        n *= s
    return n * jnp.dtype(dtype).itemsize


def _sublayer_call(body, name, sub_rows, x2d, residents, scratch, to_convert, seq=None):
    n, d = x2d.shape
    rows = BLOCK_ROWS
    n_steps = n // rows
    if seq is None:
        grid, semantics = (n_steps,), ("parallel",)
        flat = lambda i: i
    else:
        blocks = seq // rows
        grid, semantics = (n // seq, blocks), ("parallel", "arbitrary")
        flat = lambda b, t: b * blocks + t
    row_spec = pl.BlockSpec((rows, d), lambda *idx: (flat(*idx), 0))

    def conv_spec(stacked, layer, squeezed_layer_axis):
        block, col_parts = _step_slicing(stacked.shape[1:], n_steps)
        if squeezed_layer_axis:
            return pl.BlockSpec((None,) + block, lambda *idx: (
                layer, flat(*idx) // col_parts, flat(*idx) % col_parts))
        return pl.BlockSpec(block, lambda *idx: (flat(*idx) // col_parts, flat(*idx) % col_parts))

    conv_specs = [conv_spec(w, layer, True) for w, layer in to_convert]

    io = 2 * 2 * _nbytes((rows, d), _F32)
    conv = sum(2 * (_nbytes(s.block_shape[1:], _F32) + _nbytes(s.block_shape[1:], _BF16))
               for s in conv_specs)
    temps = (rows // sub_rows) * (4 * _nbytes((sub_rows, 2 * V7X_MXU_DIM), _F32)
                                  + 2 * _nbytes((sub_rows, d), _F32))
    vmem = (sum(_nbytes(r.shape, r.dtype) for r in residents)
            + sum(_nbytes(s, t) for s, t in scratch) + io + conv + temps)
    assert vmem <= V7X_VMEM_BYTES, vmem

    outs = pl.pallas_call(
        functools.partial(_convert_then, body, 1 + len(residents), len(to_convert)),
        grid=grid,
        in_specs=[row_spec] + [_resident(r.shape) for r in residents] + conv_specs,
        out_specs=[row_spec] + [conv_spec(w, layer, False) for w, layer in to_convert],
        out_shape=[jax.ShapeDtypeStruct((n, d), x2d.dtype)]
                  + [jax.ShapeDtypeStruct(w.shape[1:], _BF16) for w, _ in to_convert],
        scratch_shapes=[pltpu.VMEM(s, t) for s, t in scratch],
        compiler_params=pltpu.CompilerParams(
            dimension_semantics=semantics, vmem_limit_bytes=vmem),
        name=name,
    )(x2d, *residents, *[w for w, _ in to_convert])
    return outs[0], list(outs[1:])


def kernel(x, a_w_in, a_ln_g, a_ln_b, a_w_s, a_b_s, a_w_out, b_w_in, b_w_grp, b_scale, b_w_out, mix_pre_g, mix_post_g, ffn_pre_g, ffn_post_g, ffn_w_gate, ffn_w_up, ffn_w_down):
    bsz, seq, d = x.shape
    depth = mix_pre_g.shape[0]
    rows = BLOCK_ROWS
    a_width = a_w_out.shape[1]
    b_width = b_w_out.shape[1]
    b_gdim = b_w_grp.shape[2]
    d_ff = ffn_w_down.shape[1]
    assert seq % rows == 0 and rows % POOL_SUB_ROWS == 0 and rows % SUB_ROWS == 0
    assert SUB_ROWS % a_w_s.shape[-1] == 0
    assert len(B_WINDOWS) == b_w_grp.shape[1] and max(B_WINDOWS) <= POOL_HALO - V7X_SUBLANES
    assert (a_width // a_w_s.shape[1]) % V7X_MXU_DIM == 0
    n_sub_b = rows // POOL_SUB_ROWS
    b_w_grp_rows = b_w_grp.reshape(b_w_grp.shape[0], b_width, b_gdim)

    def mixer_a(i, j):
        smalls = [mix_pre_g[i][None, :], mix_post_g[i][None, :]]
        mids = [a_ln_g[j][None, :], a_ln_b[j][None, :], a_w_s[j], jnp.transpose(a_b_s[j])]
        scratch = [((rows, d), _BF16), ((rows, a_width), _F32), ((rows, a_width), _BF16)]
        def call(x2d, w, nxt):
            return _sublayer_call(_mixer_a_kernel, "mixer_a_sublayer", SUB_ROWS, x2d,
                                  smalls + [w[0]] + mids + [w[1]], scratch, nxt)
        return [(a_w_in, j), (a_w_out, j)], call

    def mixer_b(i, j):
        smalls = [mix_pre_g[i][None, :], mix_post_g[i][None, :]]
        scratch = [((rows + POOL_HALO, b_width), _F32),
                   ((n_sub_b * (POOL_SUB_ROWS + POOL_HALO), b_gdim), _F32),
                   ((n_sub_b * (POOL_SUB_ROWS + POOL_HALO), b_gdim), _F32),
                   ((rows, b_width), _BF16)]
        def call(x2d, w, nxt):
            return _sublayer_call(_mixer_b_kernel, "mixer_b_sublayer", POOL_SUB_ROWS, x2d,
                                  smalls + [w[0], w[1], b_scale[j][None, :], w[2]], scratch, nxt,
                                  seq=seq)
        return [(b_w_in, j), (b_w_grp_rows, j), (b_w_out, j)], call

    def ffn(i):
        smalls = [ffn_pre_g[i][None, :], ffn_post_g[i][None, :]]
        scratch = [((rows, d), _BF16), ((rows, d_ff), _BF16)]
        def call(x2d, w, nxt):
            return _sublayer_call(_ffn_kernel, "ffn_sublayer", SUB_ROWS, x2d,
                                  smalls + list(w), scratch, nxt)
        return [(ffn_w_gate, i), (ffn_w_up, i), (ffn_w_down, i)], call

    sublayers = []
    for i in range(depth):
        sublayers.append(mixer_a(i, i // 2) if i % 2 == 0 else mixer_b(i, i // 2))
        sublayers.append(ffn(i))

    x2d = x.reshape(bsz * seq, d)
    weights = [w[layer].astype(_BF16) for w, layer in sublayers[0][0]]
    for k, (_, call) in enumerate(sublayers):
        nxt = sublayers[k + 1][0] if k + 1 < len(sublayers) else []
        x2d, weights = call(x2d, weights, nxt)
    return x2d.reshape(bsz, seq, d)
```
